```python
import math
import jax, jax.numpy as jnp
from jax import lax
import numpy as np

D_MODEL = 1024
BATCH = 16
SEQ = 2048
DEPTH = 2
DEC_BATCH = 8
DEC_SEQ = 16
PAST_LEN = 2048

CHUNK = 64
Q_BLOCK = 128
M_EXPAND = 2
D_INNER = M_EXPAND * D_MODEL
M_HEADDIM = 64
M_HEADS = D_INNER // M_HEADDIM
M_GROUPS = 8
M_HPG = M_HEADS // M_GROUPS
D_STATE = 128
CONV_K = 4
CONV_DIM = D_INNER + 2 * M_GROUPS * D_STATE
SSD_CHUNK = CHUNK
A_HEADS = 16
A_HEADDIM = 64
D_ATTN = A_HEADS * A_HEADDIM
IN_SIZES = (D_INNER, CONV_DIM, M_HEADS, D_ATTN, D_ATTN, D_ATTN, A_HEADS, D_ATTN, D_MODEL, D_MODEL)
D_IN = D_INNER + CONV_DIM + M_HEADS + 4 * D_ATTN + A_HEADS + 2 * D_MODEL
DN_ALPHA = (2 * DEPTH) ** 0.25
DN_BETA = (8 * DEPTH) ** -0.25
LN_EPS = 1e-5
RMS_EPS = 1e-5

kernel_name = 'hybrid_ssd_fox_stream_step'


def split_projection(h):
    idx = [int(i) for i in np.cumsum(IN_SIZES)[:-1]]
    return jnp.split(h, idx, axis=-1)


def layer_norm(x, g, b):
    xf = x.astype(jnp.float32)
    mu = jnp.mean(xf, axis=-1, keepdims=True)
    var = jnp.mean(jnp.square(xf - mu), axis=-1, keepdims=True)
    y = (xf - mu) * lax.rsqrt(var + LN_EPS) * g.astype(jnp.float32) + b.astype(jnp.float32)
    return y.astype(x.dtype)


def causal_conv(xbc, conv_state, w, bias):
    L = xbc.shape[1]
    xpad = jnp.concatenate([conv_state.astype(xbc.dtype), xbc], axis=1)
    out = bias
    for j in range(CONV_K):
        out = out + xpad[:, j:j + L] * w[j]
    return jax.nn.silu(out), xpad[:, xpad.shape[1] - (CONV_K - 1):]


def ssd_scan(xs, dt, A, Bm, Cm, s0, chunk):
    b, L = xs.shape[:2]
    nc = L // chunk
    xc = xs.reshape(b, nc, chunk, M_GROUPS, M_HPG, M_HEADDIM)
    dtc = dt.reshape(b, nc, chunk, M_GROUPS, M_HPG)
    Bc = Bm.reshape(b, nc, chunk, M_GROUPS, D_STATE)
    Cc = Cm.reshape(b, nc, chunk, M_GROUPS, D_STATE)
    acum = jnp.cumsum(dtc * A, axis=2)
    seg = acum[:, :, :, None] - acum[:, :, None]
    causal = jnp.tril(jnp.ones((chunk, chunk), dtype=bool))[:, :, None, None]
    decay = jnp.exp(jnp.where(causal, seg, -jnp.inf))
    cb = jnp.einsum('bclgn,bcsgn->bclsg', Cc, Bc)
    m = cb[..., None] * decay * dtc[:, :, None]
    y_diag = jnp.einsum('bclsgr,bcsgrp->bclgrp', m, xc)

    def step(s_in, inp):
        x_c, dt_c, b_c, c_c, acum_c = inp
        y_off = jnp.einsum('blgn,bgrpn->blgrp', c_c, s_in) * jnp.exp(acum_c)[..., None]
        a_end = acum_c[:, -1]
        w_end = jnp.exp(a_end[:, None] - acum_c) * dt_c
        s_c = jnp.einsum('blgn,blgr,blgrp->bgrpn', b_c, w_end, x_c)
        s_out = jnp.exp(a_end)[..., None, None] * s_in + s_c
        return s_out, y_off

    mv = lambda a: jnp.moveaxis(a, 1, 0)
    s_fin, y_off = lax.scan(step, s0, (mv(xc), mv(dtc), mv(Bc), mv(Cc), mv(acum)))
    y = y_diag + jnp.moveaxis(y_off, 0, 1)
    return y.reshape(b, L, M_GROUPS, M_HPG, M_HEADDIM), s_fin


def fox_block(q, k, v, cq, ck, q_pos):
    s = jnp.einsum('bqhd,bkhd->bhqk', q, k).astype(jnp.float32) * (A_HEADDIM ** -0.5)
    bias = jnp.moveaxis(cq, 1, 2)[..., :, None] - jnp.moveaxis(ck, 1, 2)[..., None, :]
    mask = jnp.arange(k.shape[1])[None, :] <= q_pos[:, None]
    p = jax.nn.softmax(jnp.where(mask, s + bias, -jnp.inf), axis=-1)
    return jnp.einsum('bhqk,bkhd->bqhd', p.astype(v.dtype), v)


def fox_prompt(q, k, v, logf):
    b, L = q.shape[:2]
    nb = L // Q_BLOCK
    c = jnp.cumsum(logf.astype(jnp.float32), axis=1)
    qb = jnp.moveaxis(q.reshape(b, nb, Q_BLOCK, A_HEADS, A_HEADDIM), 1, 0)
    cb = jnp.moveaxis(c.reshape(b, nb, Q_BLOCK, A_HEADS), 1, 0)
    pos = jnp.arange(L).reshape(nb, Q_BLOCK)
    out = lax.map(lambda a: fox_block(a[0], k, v, a[1], c, a[2]), (qb, cb, pos))
    return jnp.moveaxis(out, 0, 1).reshape(b, L, A_HEADS, A_HEADDIM)


def fox_sample(q, k, v, logf, cache_k, cache_v, cache_logf):
    past = cache_k.shape[1]
    k_all = jnp.concatenate([cache_k.astype(k.dtype), k], axis=1)
    v_all = jnp.concatenate([cache_v.astype(v.dtype), v], axis=1)
    c_all = jnp.cumsum(jnp.concatenate([cache_logf.astype(jnp.float32), logf.astype(jnp.float32)], axis=1), axis=1)
    q_pos = past + jnp.arange(q.shape[1])
    return fox_block(q, k_all, v_all, c_all[:, past:], c_all, q_pos)


def mixer_layer(x, conv_state, ssm_state, past_kvf, chunk, w_in, conv_w, conv_b, dt_bias, a_log,
                d_skip, mnorm_w, b_f, b_merge, w_br_m, w_br_a, w_out, ln_g, ln_b):
    b, L = x.shape[:2]
    f32 = jnp.float32
    h = jnp.einsum('bld,de->ble', x, w_in)
    z, xbc, dt_raw, q, k, v, f_raw, g_path, mg_m, mg_a = split_projection(h)

    xbc, new_conv = causal_conv(xbc, conv_state, conv_w, conv_b)
    xs, Bm, Cm = jnp.split(xbc, [D_INNER, D_INNER + M_GROUPS * D_STATE], axis=-1)
    xs = xs.reshape(b, L, M_GROUPS, M_HPG, M_HEADDIM).astype(f32)
    Bm = Bm.reshape(b, L, M_GROUPS, D_STATE).astype(f32)
    Cm = Cm.reshape(b, L, M_GROUPS, D_STATE).astype(f32)
    dt = jax.nn.softplus(dt_raw.astype(f32) + dt_bias.astype(f32)).reshape(b, L, M_GROUPS, M_HPG)
    A = -jnp.exp(a_log.astype(f32)).reshape(M_GROUPS, M_HPG)
    s0 = ssm_state.astype(f32).reshape(b, M_GROUPS, M_HPG, M_HEADDIM, D_STATE)
    y, new_ssm = ssd_scan(xs, dt, A, Bm, Cm, s0, chunk)
    y = y + d_skip.astype(f32).reshape(M_GROUPS, M_HPG)[:, :, None] * xs
    y = y.reshape(b, L, D_INNER) * jax.nn.silu(z.astype(f32))
    yg = y.reshape(b, L, M_GROUPS, D_INNER // M_GROUPS)
    yg = yg * lax.rsqrt(jnp.mean(jnp.square(yg), axis=-1, keepdims=True) + RMS_EPS)
    y_m = (yg.reshape(b, L, D_INNER) * mnorm_w.astype(f32)).astype(x.dtype)

    q = q.reshape(b, L, A_HEADS, A_HEADDIM)
    k = k.reshape(b, L, A_HEADS, A_HEADDIM)
    v = v.reshape(b, L, A_HEADS, A_HEADDIM)
    logf = jax.nn.log_sigmoid((f_raw + b_f).astype(f32))
    if past_kvf is None:
        o = fox_prompt(q, k, v, logf)
    else:
        o = fox_sample(q, k, v, logf, past_kvf[0], past_kvf[1], past_kvf[2])
    y_a = o.reshape(b, L, D_ATTN) * jax.nn.silu(g_path)

    gate_m = jax.nn.sigmoid(mg_m + b_merge[:D_MODEL])
    gate_a = jax.nn.sigmoid(mg_a + b_merge[D_MODEL:])
    u = gate_m * jnp.einsum('ble,ed->bld', y_m, w_br_m) + gate_a * jnp.einsum('ble,ed->bld', y_a, w_br_a)
    out = jnp.einsum('bld,de->ble', u, w_out)
    x_new = layer_norm(DN_ALPHA * x + out, ln_g, ln_b)
    return x_new, (k, v, logf, new_ssm.reshape(b, M_HEADS, M_HEADDIM, D_STATE), new_conv)


def setup_inputs(seed: int = 0) -> dict:
    key = jax.random.key(seed)
    ks = jax.random.split(key, 24)
    nrm = jax.random.normal
    col_scale = jnp.concatenate([
        jnp.ones((D_INNER + CONV_DIM + M_HEADS + 2 * D_ATTN,), jnp.float32),
        jnp.full((D_ATTN,), DN_BETA, jnp.float32),
        jnp.ones((A_HEADS + D_ATTN + 2 * D_MODEL,), jnp.float32)])
    w_in = nrm(ks[7], (DEPTH, D_MODEL, D_IN), jnp.float32) * (D_MODEL ** -0.5) * col_scale
    u = jax.random.uniform(ks[10], (DEPTH, M_HEADS), jnp.float32)
    dt0 = jnp.exp(u * (math.log(0.1) - math.log(1e-3)) + math.log(1e-3))
    dt_bias = dt0 + jnp.log(-jnp.expm1(-dt0))
    return {
        'x_prompt': nrm(ks[0], (BATCH, SEQ, D_MODEL), jnp.float32),
        'x_sample': nrm(ks[1], (DEC_BATCH, DEC_SEQ, D_MODEL), jnp.float32),
        'cache_k': nrm(ks[2], (DEPTH, DEC_BATCH, PAST_LEN, A_HEADS, A_HEADDIM), jnp.float32),
        'cache_v': DN_BETA * nrm(ks[3], (DEPTH, DEC_BATCH, PAST_LEN, A_HEADS, A_HEADDIM), jnp.float32),
        'cache_logf': jax.nn.log_sigmoid(1.0 + nrm(ks[4], (DEPTH, DEC_BATCH, PAST_LEN, A_HEADS), jnp.float32)),
        'state_ssm': 0.1 * nrm(ks[5], (DEPTH, DEC_BATCH, M_HEADS, M_HEADDIM, D_STATE), jnp.float32),
        'state_conv': nrm(ks[6], (DEPTH, DEC_BATCH, CONV_K - 1, CONV_DIM), jnp.float32),
        'w_in': w_in,
        'conv_w': nrm(ks[8], (DEPTH, CONV_K, CONV_DIM), jnp.float32) * (CONV_K ** -0.5),
        'conv_b': 0.01 * nrm(ks[9], (DEPTH, CONV_DIM), jnp.float32),
        'dt_bias': dt_bias,
        'a_log': jnp.log(jax.random.uniform(ks[11], (DEPTH, M_HEADS), jnp.float32, 1.0, 16.0)),
        'd_skip': 1.0 + 0.1 * nrm(ks[12], (DEPTH, M_HEADS), jnp.float32),
        'mnorm_w': 1.0 + 0.1 * nrm(ks[13], (DEPTH, D_INNER), jnp.float32),
        'b_f': 1.0 + 0.5 * nrm(ks[14], (DEPTH, A_HEADS), jnp.float32),
        'b_merge': 0.1 * nrm(ks[15], (DEPTH, 2 * D_MODEL), jnp.float32),
        'w_br_m': nrm(ks[16], (DEPTH, D_INNER, D_MODEL), jnp.float32) * (D_INNER ** -0.5) * DN_BETA,
        'w_br_a': nrm(ks[17], (DEPTH, D_ATTN, D_MODEL), jnp.float32) * (D_ATTN ** -0.5) * DN_BETA,
        'w_out': nrm(ks[18], (DEPTH, D_MODEL, D_MODEL), jnp.float32) * (D_MODEL ** -0.5) * DN_BETA,
        'ln_g': 1.0 + 0.1 * nrm(ks[19], (DEPTH, D_MODEL), jnp.float32),
        'ln_b': 0.01 * nrm(ks[20], (DEPTH, D_MODEL), jnp.float32),
    }


def reference(x_prompt, x_sample, cache_k, cache_v, cache_logf, state_ssm, state_conv, w_in, conv_w,
              conv_b, dt_bias, a_log, d_skip, mnorm_w, b_f, b_merge, w_br_m, w_br_a, w_out, ln_g, ln_b):
    xp = x_prompt
    xs = x_sample
    bp = xp.shape[0]
    pk, pv, pf, pssm, pconv = [], [], [], [], []
    sk, sv, sf, sssm, sconv = [], [], [], [], []
    for l in range(DEPTH):
        lw = (w_in[l], conv_w[l], conv_b[l], dt_bias[l], a_log[l], d_skip[l], mnorm_w[l], b_f[l],
              b_merge[l], w_br_m[l], w_br_a[l], w_out[l], ln_g[l], ln_b[l])
        conv0 = jnp.zeros((bp, CONV_K - 1, CONV_DIM), xp.dtype)
        ssm0 = jnp.zeros((bp, M_HEADS, M_HEADDIM, D_STATE), jnp.float32)
        xp, st_p = mixer_layer(xp, conv0, ssm0, None, SSD_CHUNK, *lw)
        xs, st_s = mixer_layer(xs, state_conv[l], state_ssm[l], (cache_k[l], cache_v[l], cache_logf[l]),
                               xs.shape[1], *lw)
        pk.append(st_p[0]); pv.append(st_p[1]); pf.append(st_p[2]); pssm.append(st_p[3]); pconv.append(st_p[4])
        sk.append(st_s[0]); sv.append(st_s[1]); sf.append(st_s[2]); sssm.append(st_s[3]); sconv.append(st_s[4])
    new_k_p = jnp.stack(pk)
    new_v_p = jnp.stack(pv)
    new_logf_p = jnp.stack(pf)
    new_ssm_p = jnp.stack(pssm)
    new_conv_p = jnp.stack(pconv)
    new_k_s = jnp.stack(sk)
    new_v_s = jnp.stack(sv)
    new_logf_s = jnp.stack(sf)
    new_ssm_s = jnp.stack(sssm)
    new_conv_s = jnp.stack(sconv)
    return (xp, xs, new_k_p, new_v_p, new_logf_p, new_ssm_p, new_conv_p,
            new_k_s, new_v_s, new_logf_s, new_ssm_s, new_conv_s)
```

```python
import functools
import math

import jax
import jax.numpy as jnp
from jax import lax
from jax.experimental import pallas as pl
from jax.experimental.pallas import tpu as pltpu

D_MODEL = 1024
DEPTH = 2
D_INNER = 2048
M_HEADDIM = 64
M_HEADS = 32
M_GROUPS = 8
M_HPG = 4
D_STATE = 128
CONV_K = 4
CONV_DIM = 4096
A_HEADS = 16
A_HEADDIM = 64
D_ATTN = 1024
DN_ALPHA = (2 * DEPTH) ** 0.25
LN_EPS = 1e-5
RMS_EPS = 1e-5

LANES = 128
SSD_L = 128
ATT_T = 256
GROUP_W = M_HPG * M_HEADDIM
F_LANE0 = M_HEADS
NEG_BIG = -1e30

F32 = jnp.float32
BF16 = jnp.bfloat16
HI = lax.Precision.HIGHEST


def _cparams(sem, vmem_mb):
    return pltpu.CompilerParams(dimension_semantics=sem, vmem_limit_bytes=vmem_mb * 1024 * 1024)


def _mm_kernel(x_ref, w_ref, o_ref):
    o_ref[...] = jnp.dot(x_ref[...], w_ref[...], preferred_element_type=F32).astype(o_ref.dtype)


def _matmul(x, w, out_dtype, name):
    t, k = x.shape
    n = w.shape[1]
    tm = min(t, 1024)
    tn = min(n, 1024)
    return pl.pallas_call(
        _mm_kernel,
        grid=(n // tn, t // tm),
        in_specs=[pl.BlockSpec((tm, k), lambda j, i: (i, 0)),
                  pl.BlockSpec((k, tn), lambda j, i: (0, j))],
        out_specs=pl.BlockSpec((tm, tn), lambda j, i: (i, j)),
        out_shape=jax.ShapeDtypeStruct((t, n), out_dtype),
        compiler_params=_cparams(("arbitrary", "arbitrary"), 40),
        name=name,
    )(x, w)


def _softplus(x):
    return jnp.maximum(x, 0.0) + jnp.log1p(jnp.exp(-jnp.abs(x)))


def _silu(x):
    return x * jax.nn.sigmoid(x)


def _ssd_kernel(valid, z_ref, x_ref, bc_ref, dtf_ref, convw_ref, convb_ref, dtb_ref, alog_ref,
                dskip_ref, mnw_ref, bf_ref, conv0_ref, ssm0_ref,
                ym_ref, lf_ref, nconv_ref, nssm_ref,
                xpad_ref, xc_ref, y_ref, xw_ref, st_ref):
    c = pl.program_id(1)
    n_l = x_ref.shape[0]

    @pl.when(c == 0)
    def _init():
        xpad_ref[0:8, :] = jnp.zeros((8, CONV_DIM), F32)
        xpad_ref[5:8, :] = conv0_ref[0]
        for g in range(M_GROUPS):
            st_ref[g] = ssm0_ref[0, g].T

    xpad_ref[8:8 + n_l, 0:D_INNER] = x_ref[...].astype(F32)
    xpad_ref[8:8 + n_l, D_INNER:CONV_DIM] = bc_ref[...].astype(F32)
    cb = 512
    for j0 in range(0, CONV_DIM, cb):
        acc = jnp.broadcast_to(convb_ref[:, j0:j0 + cb], (n_l, cb))
        for j in range(CONV_K):
            acc = acc + xpad_ref[5 + j:5 + j + n_l, j0:j0 + cb] * convw_ref[j:j + 1, j0:j0 + cb]
        xc_ref[:, j0:j0 + cb] = _silu(acc)
    tail = xpad_ref[8 + valid - 3:8 + valid, :]
    xpad_ref[5:8, :] = tail
    nconv_ref[0] = tail

    dtf = dtf_ref[...]
    lf_ref[...] = -_softplus(-(dtf + bf_ref[...]))
    dt = _softplus(dtf + dtb_ref[...])
    row = lax.broadcasted_iota(jnp.int32, (n_l, n_l), 0)
    col = lax.broadcasted_iota(jnp.int32, (n_l, n_l), 1)
    if valid < n_l:
        dt = jnp.where(lax.broadcasted_iota(jnp.int32, (n_l, LANES), 0) < valid, dt, 0.0)
    a = dt * (-jnp.exp(alog_ref[...]))
    causal = row >= col
    acum = jnp.dot(causal.astype(F32), a, precision=HI, preferred_element_type=F32)
    acum_t = acum.T
    a_end = acum[n_l - 1:n_l, :]
    e_end = jnp.exp(a_end)

    for g in range(M_GROUPS):
        b_g = xc_ref[:, D_INNER + g * D_STATE:D_INNER + (g + 1) * D_STATE].astype(BF16)
        c_g = xc_ref[:, D_INNER + M_GROUPS * D_STATE + g * D_STATE:
                     D_INNER + M_GROUPS * D_STATE + (g + 1) * D_STATE].astype(BF16)
        gmat = lax.dot_general(c_g, b_g, (((1,), (1,)), ((), ())), preferred_element_type=F32)
        st_g = st_ref[g]
        yoff = jnp.dot(c_g, st_g.astype(BF16), preferred_element_type=F32)
        dec_cols = []
        for r in range(M_HPG):
            h = g * M_HPG + r
            lo = g * GROUP_W + r * M_HEADDIM
            colv = acum[:, h:h + 1]
            rowv = acum_t[h:h + 1, :]
            dec = jnp.exp(jnp.where(causal, colv - rowv, NEG_BIG))
            m = (gmat * dec).astype(BF16)
            xh = xc_ref[:, lo:lo + M_HEADDIM]
            xdt = xh * dt[:, h:h + 1]
            yd = jnp.dot(m, xdt.astype(BF16), preferred_element_type=F32)
            y = (yd + yoff[:, r * M_HEADDIM:(r + 1) * M_HEADDIM] * jnp.exp(colv)
                 + dskip_ref[:, h:h + 1] * xh)
            y_ref[:, lo:lo + M_HEADDIM] = y
            xw_ref[:, r * M_HEADDIM:(r + 1) * M_HEADDIM] = xdt * jnp.exp(a_end[:, h:h + 1] - colv)
            dec_cols.append(jnp.broadcast_to(e_end[:, h:h + 1], (1, M_HEADDIM)))
        dec_row = jnp.concatenate(dec_cols, axis=1)
        upd = lax.dot_general(b_g, xw_ref[...].astype(BF16), (((0,), (0,)), ((), ())),
                              preferred_element_type=F32)
        st_ref[g] = st_g * dec_row + upd

    for g in range(M_GROUPS):
        sl = slice(g * GROUP_W, (g + 1) * GROUP_W)
        yg = y_ref[:, sl] * _silu(z_ref[:, sl].astype(F32))
        ms = jnp.mean(yg * yg, axis=-1, keepdims=True)
        ym_ref[:, sl] = (yg * lax.rsqrt(ms + RMS_EPS) * mnw_ref[:, sl]).astype(ym_ref.dtype)

    @pl.when(c == pl.num_programs(1) - 1)
    def _fin():
        for g in range(M_GROUPS):
            nssm_ref[0, g] = st_ref[g].T


def _ssd(zx, dtf, conv0, ssm0, lw, n_b, n_s, valid, name):
    n_c = n_s // SSD_L
    tok = lambda b, c: (b * n_c + c, 0)
    vec = lambda b, c: (0, 0)
    ssm0_g = ssm0.reshape(n_b, M_GROUPS, GROUP_W, D_STATE)
    outs = pl.pallas_call(
        functools.partial(_ssd_kernel, valid),
        grid=(n_b, n_c),
        in_specs=[
            pl.BlockSpec((SSD_L, D_INNER), lambda b, c: (b * n_c + c, 0)),
            pl.BlockSpec((SSD_L, D_INNER), lambda b, c: (b * n_c + c, 1)),
            pl.BlockSpec((SSD_L, D_INNER), lambda b, c: (b * n_c + c, 2)),
            pl.BlockSpec((SSD_L, LANES), tok),
            pl.BlockSpec((CONV_K, CONV_DIM), vec),
            pl.BlockSpec((1, CONV_DIM), vec),
            pl.BlockSpec((1, LANES), vec),
            pl.BlockSpec((1, LANES), vec),
            pl.BlockSpec((1, LANES), vec),
            pl.BlockSpec((1, D_INNER), vec),
            pl.BlockSpec((1, LANES), vec),
            pl.BlockSpec((1, CONV_K - 1, CONV_DIM), lambda b, c: (b, 0, 0)),
            pl.BlockSpec((1, M_GROUPS, GROUP_W, D_STATE), lambda b, c: (b, 0, 0, 0)),
        ],
        out_specs=[
            pl.BlockSpec((SSD_L, D_INNER), tok),
            pl.BlockSpec((SSD_L, LANES), tok),
            pl.BlockSpec((1, CONV_K - 1, CONV_DIM), lambda b, c: (b, 0, 0)),
            pl.BlockSpec((1, M_GROUPS, GROUP_W, D_STATE), lambda b, c: (b, 0, 0, 0)),
        ],
        out_shape=[
            jax.ShapeDtypeStruct((n_b * n_s, D_INNER), BF16),
            jax.ShapeDtypeStruct((n_b * n_s, LANES), F32),
            jax.ShapeDtypeStruct((n_b, CONV_K - 1, CONV_DIM), F32),
            jax.ShapeDtypeStruct((n_b, M_GROUPS, GROUP_W, D_STATE), F32),
        ],
        scratch_shapes=[
            pltpu.VMEM((SSD_L + 8, CONV_DIM), F32),
            pltpu.VMEM((SSD_L, CONV_DIM), F32),
            pltpu.VMEM((SSD_L, D_INNER), F32),
            pltpu.VMEM((SSD_L, GROUP_W), F32),
            pltpu.VMEM((M_GROUPS, D_STATE, GROUP_W), F32),
        ],
        compiler_params=_cparams(("arbitrary", "arbitrary"), 40),
        name=name,
    )(zx, zx, zx, dtf, lw["conv_w"], lw["conv_b"], lw["dt_bias"], lw["a_log"], lw["d_skip"],
      lw["mnorm_w"], lw["b_f"], conv0, ssm0_g)
    ym, lf, nconv, nssm = outs
    return ym, lf, nconv, nssm.reshape(n_b, M_HEADS, M_HEADDIM, D_STATE)


def _cumsum_kernel(x_ref, o_ref):
    n = x_ref.shape[2]
    upper = (lax.broadcasted_iota(jnp.int32, (LANES, LANES), 0)
             <= lax.broadcasted_iota(jnp.int32, (LANES, LANES), 1)).astype(F32)
    carry = jnp.zeros((A_HEADS, 1), F32)
    for j0 in range(0, n, LANES):
        cs = jnp.dot(x_ref[0, :, j0:j0 + LANES], upper, precision=HI,
                     preferred_element_type=F32) + carry
        o_ref[0, :, j0:j0 + LANES] = cs
        carry = cs[:, LANES - 1:LANES]


def _cumsum_lanes(x, name):
    n_b, _, n = x.shape
    return pl.pallas_call(
        _cumsum_kernel,
        grid=(n_b,),
        in_specs=[pl.BlockSpec((1, A_HEADS, n), lambda b: (b, 0, 0))],
        out_specs=pl.BlockSpec((1, A_HEADS, n), lambda b: (b, 0, 0)),
        out_shape=jax.ShapeDtypeStruct(x.shape, F32),
        compiler_params=_cparams(("arbitrary",), 16),
        name=name,
    )(x)


def _head_masks(shape):
    lane = lax.broadcasted_iota(jnp.int32, shape, len(shape) - 1)
    return lane < A_HEADDIM


def _attn_p_kernel(q_ref, k_ref, v_ref, ct_ref, g_ref, o_ref, m_ref, l_ref, acc_ref):
    hp = pl.program_id(1)
    i = pl.program_id(2)
    t = ATT_T
    q = q_ref[0]
    first = _head_masks(q.shape)
    zero = jnp.zeros_like(q)
    qh = (jnp.where(first, q, zero), jnp.where(first, zero, q))
    scale = A_HEADDIM ** -0.5

    m_ref[...] = jnp.full(m_ref.shape, NEG_BIG, F32)
    l_ref[...] = jnp.zeros(l_ref.shape, F32)
    acc_ref[...] = jnp.zeros(acc_ref.shape, F32)

    def tile(j, diag):
        kt = k_ref[0, pl.ds(j * t, t), :].astype(BF16)
        vt = v_ref[0, pl.ds(j * t, t), :].astype(BF16)
        for hh in range(2):
            s = lax.dot_general(qh[hh], kt, (((1,), (1,)), ((), ())), preferred_element_type=F32)
            s = s * scale - ct_ref[0, 2 * hp + hh, pl.ds(j, 1), :]
            if diag:
                r_i = lax.broadcasted_iota(jnp.int32, (t, t), 0)
                c_i = lax.broadcasted_iota(jnp.int32, (t, t), 1)
                s = jnp.where(c_i <= r_i, s, NEG_BIG)
            m_old = m_ref[hh]
            m_new = jnp.maximum(m_old, jnp.max(s, axis=-1, keepdims=True))
            alpha = jnp.exp(m_old - m_new)
            p = jnp.exp(s - m_new)
            l_ref[hh] = alpha * l_ref[hh] + jnp.sum(p, axis=-1, keepdims=True)
            acc_ref[hh] = alpha * acc_ref[hh] + jnp.dot(p.astype(BF16), vt, preferred_element_type=F32)
            m_ref[hh] = m_new

    def body(j, carry):
        tile(j, False)
        return carry

    lax.fori_loop(0, i, body, 0)
    tile(i, True)

    o = jnp.where(first, acc_ref[0] / l_ref[0], acc_ref[1] / l_ref[1])
    o_ref[0] = (o * _silu(g_ref[0].astype(F32))).astype(o_ref.dtype)


def _attn_prompt(qg, k, v, ct, n_b, n_s, name):
    n_q = n_s // ATT_T
    n_hp = A_HEADS // 2
    ct4 = ct.reshape(n_b, A_HEADS, n_q, ATT_T)
    return pl.pallas_call(
        _attn_p_kernel,
        grid=(n_b, n_hp, n_q),
        in_specs=[
            pl.BlockSpec((1, ATT_T, LANES), lambda b, h, i: (b, i, h)),
            pl.BlockSpec((1, n_s, LANES), lambda b, h, i: (b, 0, h)),
            pl.BlockSpec((1, n_s, LANES), lambda b, h, i: (b, 0, h)),
            pl.BlockSpec((1, A_HEADS, n_q, ATT_T), lambda b, h, i: (b, 0, 0, 0)),
            pl.BlockSpec((1, ATT_T, LANES), lambda b, h, i: (b, i, n_hp + h)),
        ],
        out_specs=pl.BlockSpec((1, ATT_T, LANES), lambda b, h, i: (b, i, h)),
        out_shape=jax.ShapeDtypeStruct((n_b, n_s, D_ATTN), BF16),
        scratch_shapes=[
            pltpu.VMEM((2, ATT_T, 1), F32),
            pltpu.VMEM((2, ATT_T, 1), F32),
            pltpu.VMEM((2, ATT_T, LANES), F32),
        ],
        compiler_params=_cparams(("arbitrary", "arbitrary", "arbitrary"), 32),
        name=name,
    )(qg, k, v, ct4, qg)


def _attn_s_kernel(q_ref, kc_ref, vc_ref, kn_ref, vn_ref, ctc_ref, ctn_ref, g_ref, o_ref):
    hp = pl.program_id(1)
    q = q_ref[0]
    n_l = q.shape[0]
    first = _head_masks(q.shape)
    zero = jnp.zeros_like(q)
    qh = (jnp.where(first, q, zero), jnp.where(first, zero, q))
    scale = A_HEADDIM ** -0.5
    kc = kc_ref[0].astype(BF16)
    vc = vc_ref[0].astype(BF16)
    kn = kn_ref[0].astype(BF16)
    vn = vn_ref[0].astype(BF16)
    r_i = lax.broadcasted_iota(jnp.int32, (n_l, n_l), 0)
    c_i = lax.broadcasted_iota(jnp.int32, (n_l, n_l), 1)
    outs = []
    nt = (((1,), (1,)), ((), ()))
    for hh in range(2):
        s_c = lax.dot_general(qh[hh], kc, nt, preferred_element_type=F32) * scale
        s_c = s_c - ctc_ref[0, pl.ds(2 * hp + hh, 1), :]
        s_n = lax.dot_general(qh[hh], kn, nt, preferred_element_type=F32) * scale
        s_n = jnp.where(c_i <= r_i, s_n - ctn_ref[0, pl.ds(2 * hp + hh, 1), :], NEG_BIG)
        m = jnp.maximum(jnp.max(s_c, axis=-1, keepdims=True), jnp.max(s_n, axis=-1, keepdims=True))
        p_c = jnp.exp(s_c - m)
        p_n = jnp.exp(s_n - m)
        l = jnp.sum(p_c, axis=-1, keepdims=True) + jnp.sum(p_n, axis=-1, keepdims=True)
        o = (jnp.dot(p_c.astype(BF16), vc, preferred_element_type=F32)
             + jnp.dot(p_n.astype(BF16), vn, preferred_element_type=F32))
        outs.append(o / l)
    o = jnp.where(first, outs[0], outs[1])
    o_ref[0] = (o * _silu(g_ref[0].astype(F32))).astype(o_ref.dtype)


def _attn_sample(qg, kc, vc, kn, vn, ctc, ctn, name):
    n_b, n_l, _ = qg.shape
    n_past = kc.shape[1]
    n_hp = A_HEADS // 2
    return pl.pallas_call(
        _attn_s_kernel,
        grid=(n_b, n_hp),
        in_specs=[
            pl.BlockSpec((1, n_l, LANES), lambda b, h: (b, 0, h)),
            pl.BlockSpec((1, n_past, LANES), lambda b, h: (b, 0, h)),
            pl.BlockSpec((1, n_past, LANES), lambda b, h: (b, 0, h)),
            pl.BlockSpec((1, n_l, LANES), lambda b, h: (b, 0, h)),
            pl.BlockSpec((1, n_l, LANES), lambda b, h: (b, 0, h)),
            pl.BlockSpec((1, A_HEADS, n_past), lambda b, h: (b, 0, 0)),
            pl.BlockSpec((1, A_HEADS, n_l), lambda b, h: (b, 0, 0)),
            pl.BlockSpec((1, n_l, LANES), lambda b, h: (b, 0, n_hp + h)),
        ],
        out_specs=pl.BlockSpec((1, n_l, LANES), lambda b, h: (b, 0, h)),
        out_shape=jax.ShapeDtypeStruct((n_b, n_l, D_ATTN), BF16),
        compiler_params=_cparams(("arbitrary", "arbitrary"), 32),
        name=name,
    )(qg, kc, vc, kn, vn, ctc, ctn, qg)


def _merge_kernel(ym_ref, ya_ref, gm_ref, ga_ref, x_ref, wm_ref, wa_ref, wo_ref, bm_ref, ba_ref,
                  lng_ref, lnb_ref, o_ref, ob_ref):
    pm = jnp.dot(ym_ref[...], wm_ref[...], preferred_element_type=F32)
    pa = jnp.dot(ya_ref[...], wa_ref[...], preferred_element_type=F32)
    u = (jax.nn.sigmoid(gm_ref[...].astype(F32) + bm_ref[...]) * pm
         + jax.nn.sigmoid(ga_ref[...].astype(F32) + ba_ref[...]) * pa)
    out = jnp.dot(u.astype(BF16), wo_ref[...], preferred_element_type=F32)
    r = DN_ALPHA * x_ref[...] + out
    mu = jnp.mean(r, axis=-1, keepdims=True)
    d = r - mu
    var = jnp.mean(d * d, axis=-1, keepdims=True)
    y = d * lax.rsqrt(var + LN_EPS) * lng_ref[...] + lnb_ref[...]
    o_ref[...] = y
    ob_ref[...] = y.astype(BF16)


def _merge(ym, ya, qg, x, lw, name):
    t = x.shape[0]
    tm = min(t, 512)
    tok = lambda i: (i, 0)
    vec = lambda i: (0, 0)
    return pl.pallas_call(
        _merge_kernel,
        grid=(t // tm,),
        in_specs=[
            pl.BlockSpec((tm, D_INNER), tok),
            pl.BlockSpec((tm, D_ATTN), tok),
            pl.BlockSpec((tm, D_MODEL), lambda i: (i, 2)),
            pl.BlockSpec((tm, D_MODEL), lambda i: (i, 3)),
            pl.BlockSpec((tm, D_MODEL), tok),
            pl.BlockSpec((D_INNER, D_MODEL), vec),
            pl.BlockSpec((D_ATTN, D_MODEL), vec),
            pl.BlockSpec((D_MODEL, D_MODEL), vec),
            pl.BlockSpec((1, D_MODEL), vec),
            pl.BlockSpec((1, D_MODEL), vec),
            pl.BlockSpec((1, D_MODEL), vec),
            pl.BlockSpec((1, D_MODEL), vec),
        ],
        out_specs=[pl.BlockSpec((tm, D_MODEL), tok), pl.BlockSpec((tm, D_MODEL), tok)],
        out_shape=[jax.ShapeDtypeStruct((t, D_MODEL), F32), jax.ShapeDtypeStruct((t, D_MODEL), BF16)],
        compiler_params=_cparams(("arbitrary",), 48),
        name=name,
    )(ym, ya, qg, qg, x, lw["w_br_m"], lw["w_br_a"], lw["w_out"], lw["b_m"], lw["b_a"],
      lw["ln_g"], lw["ln_b"])


def _pad_lanes(v, lane0=0):
    out = jnp.zeros((1, LANES), F32)
    return lax.dynamic_update_slice(out, v.reshape(1, -1).astype(F32), (0, lane0))


def _layer_weights(l, w_in, conv_w, conv_b, dt_bias, a_log, d_skip, mnorm_w, b_f, b_merge,
                   w_br_m, w_br_a, w_out, ln_g, ln_b):
    w = w_in[l]
    o_dt = D_INNER + CONV_DIM
    o_q = o_dt + M_HEADS
    o_k = o_q + D_ATTN
    o_v = o_k + D_ATTN
    o_f = o_v + D_ATTN
    o_g = o_f + A_HEADS
    dtf_pad = jnp.zeros((D_MODEL, LANES - M_HEADS - A_HEADS), w.dtype)
    return {
        "w_zx": w[:, :o_dt].astype(BF16),
        "w_qg": jnp.concatenate([w[:, o_q:o_k], w[:, o_g:]], axis=1).astype(BF16),
        "w_k": w[:, o_k:o_v].astype(BF16),
        "w_v": w[:, o_v:o_f].astype(BF16),
        "w_dtf": jnp.concatenate([w[:, o_dt:o_q], w[:, o_f:o_g], dtf_pad], axis=1).astype(BF16),
        "conv_w": conv_w[l].astype(F32),
        "conv_b": conv_b[l].reshape(1, CONV_DIM).astype(F32),
        "dt_bias": _pad_lanes(dt_bias[l]),
        "a_log": _pad_lanes(a_log[l]),
        "d_skip": _pad_lanes(d_skip[l]),
        "mnorm_w": mnorm_w[l].reshape(1, D_INNER).astype(F32),
        "b_f": _pad_lanes(b_f[l], F_LANE0),
        "w_br_m": w_br_m[l].astype(BF16),
        "w_br_a": w_br_a[l].astype(BF16),
        "w_out": w_out[l].astype(BF16),
        "b_m": b_merge[l, :D_MODEL].reshape(1, D_MODEL).astype(F32),
        "b_a": b_merge[l, D_MODEL:].reshape(1, D_MODEL).astype(F32),
        "ln_g": ln_g[l].reshape(1, D_MODEL).astype(F32),
        "ln_b": ln_b[l].reshape(1, D_MODEL).astype(F32),
    }


def _in_proj(xb, lw, tag):
    zx = _matmul(xb, lw["w_zx"], BF16, "inproj_zx_" + tag)
    qg = _matmul(xb, lw["w_qg"], BF16, "inproj_qg_" + tag)
    k = _matmul(xb, lw["w_k"], F32, "inproj_k_" + tag)
    v = _matmul(xb, lw["w_v"], F32, "inproj_v_" + tag)
    dtf = _matmul(xb, lw["w_dtf"], F32, "inproj_dtf_" + tag)
    return zx, qg, k, v, dtf


def _logf_heads(lf, n_b, n_s):
    return lf.reshape(n_b, n_s, LANES)[:, :, F_LANE0:F_LANE0 + A_HEADS]


def kernel(x_prompt, x_sample, cache_k, cache_v, cache_logf, state_ssm, state_conv, w_in, conv_w,
           conv_b, dt_bias, a_log, d_skip, mnorm_w, b_f, b_merge, w_br_m, w_br_a, w_out, ln_g, ln_b):
    n_bp, n_sp, _ = x_prompt.shape
    n_bs, n_ls, _ = x_sample.shape
    n_past = cache_k.shape[2]
    n_lpad = SSD_L
    assert n_sp % SSD_L == 0 and n_sp % ATT_T == 0 and n_ls <= n_lpad

    xp = x_prompt.reshape(n_bp * n_sp, D_MODEL)
    xs = jnp.pad(x_sample, ((0, 0), (0, n_lpad - n_ls), (0, 0))).reshape(n_bs * n_lpad, D_MODEL)
    xp_b = xp.astype(BF16)
    xs_b = xs.astype(BF16)
    conv0_p = jnp.zeros((n_bp, CONV_K - 1, CONV_DIM), F32)
    ssm0_p = jnp.zeros((n_bp, M_HEADS, M_HEADDIM, D_STATE), F32)

    outs = {n: [] for n in ("pk", "pv", "pf", "pssm", "pconv", "sk", "sv", "sf", "sssm", "sconv")}
    for l in range(DEPTH):
        lw = _layer_weights(l, w_in, conv_w, conv_b, dt_bias, a_log, d_skip, mnorm_w, b_f, b_merge,
                            w_br_m, w_br_a, w_out, ln_g, ln_b)
        tag = str(l)

        zx, qg, k, v, dtf = _in_proj(xp_b, lw, "p" + tag)
        ym, lf, nconv, nssm = _ssd(zx, dtf, conv0_p, ssm0_p, lw, n_bp, n_sp, SSD_L, "ssd_p" + tag)
        logf = _logf_heads(lf, n_bp, n_sp)
        ct = _cumsum_lanes(jnp.swapaxes(logf, 1, 2), "cumsum_p" + tag)
        ya = _attn_prompt(qg.reshape(n_bp, n_sp, -1), k.reshape(n_bp, n_sp, D_ATTN),
                          v.reshape(n_bp, n_sp, D_ATTN), ct, n_bp, n_sp, "attn_p" + tag)
        xp, xp_b = _merge(ym, ya.reshape(n_bp * n_sp, D_ATTN), qg, xp, lw, "merge_p" + tag)
        outs["pk"].append(k.reshape(n_bp, n_sp, A_HEADS, A_HEADDIM))
        outs["pv"].append(v.reshape(n_bp, n_sp, A_HEADS, A_HEADDIM))
        outs["pf"].append(logf)
        outs["pssm"].append(nssm)
        outs["pconv"].append(nconv)

        zx, qg, k, v, dtf = _in_proj(xs_b, lw, "s" + tag)
        ym, lf, nconv, nssm = _ssd(zx, dtf, state_conv[l].astype(F32), state_ssm[l].astype(F32), lw,
                                   n_bs, n_lpad, n_ls, "ssd_s" + tag)
        logf = _logf_heads(lf, n_bs, n_lpad)
        lf_all = jnp.concatenate([jnp.swapaxes(cache_logf[l].astype(F32), 1, 2),
                                  jnp.swapaxes(logf, 1, 2)], axis=2)
        ct = _cumsum_lanes(lf_all, "cumsum_s" + tag)
        k3 = k.reshape(n_bs, n_lpad, D_ATTN)
        v3 = v.reshape(n_bs, n_lpad, D_ATTN)
        ya = _attn_sample(qg.reshape(n_bs, n_lpad, -1),
                          cache_k[l].reshape(n_bs, n_past, D_ATTN),
                          cache_v[l].reshape(n_bs, n_past, D_ATTN),
                          k3, v3, ct[:, :, :n_past], ct[:, :, n_past:], "attn_s" + tag)
        xs, xs_b = _merge(ym, ya.reshape(n_bs * n_lpad, D_ATTN), qg, xs, lw, "merge_s" + tag)
        outs["sk"].append(k3[:, :n_ls].reshape(n_bs, n_ls, A_HEADS, A_HEADDIM))
        outs["sv"].append(v3[:, :n_ls].reshape(n_bs, n_ls, A_HEADS, A_HEADDIM))
        outs["sf"].append(logf[:, :n_ls])
        outs["sssm"].append(nssm)
        outs["sconv"].append(nconv)

    y_prompt = xp.reshape(n_bp, n_sp, D_MODEL)
    y_sample = xs.reshape(n_bs, n_lpad, D_MODEL)[:, :n_ls]
    st = lambda n: jnp.stack(outs[n])
    return (y_prompt, y_sample, st("pk"), st("pv"), st("pf"), st("pssm"), st("pconv"),
            st("sk"), st("sv"), st("sf"), st("sssm"), st("sconv"))
```
